```python
import jax, jax.numpy as jnp
from jax import lax
import numpy as np

D_MODEL = 2048
BATCH = 1
SEQ = 16384
DEPTH = 1

HEAD_DIM = 128
N_MLSTM_HEADS = D_MODEL // 2 // HEAD_DIM
N_RET_HEADS = D_MODEL // 2 // HEAD_DIM
D_MLSTM = N_MLSTM_HEADS * HEAD_DIM
D_RET = N_RET_HEADS * HEAD_DIM
D_MIX = D_MLSTM + D_RET
D_PROJ = 4 * D_MLSTM + 2 * N_MLSTM_HEADS + 4 * D_RET
CHUNK = 128
CONV_WIDTH = 4
ROPE_BASE = 10000.0
D_FF = -(-(8 * D_MODEL) // (3 * 256)) * 256
NORM_EPS = 1e-6

kernel_name = "hybrid_mlstm_retention_sandwich_block"

SPLITS = list(np.cumsum([D_MLSTM, D_MLSTM, D_MLSTM, D_MLSTM, N_MLSTM_HEADS, N_MLSTM_HEADS, D_RET, D_RET, D_RET]))


def rmsnorm(x, g):
    xf = x.astype(jnp.float32)
    y = xf * lax.rsqrt(jnp.mean(xf * xf, axis=-1, keepdims=True) + NORM_EPS)
    return (y * g.astype(jnp.float32)).astype(x.dtype)


def head_rmsnorm(h, g):
    B, S, H, D = h.shape
    y = h * lax.rsqrt(jnp.mean(h * h, axis=-1, keepdims=True) + NORM_EPS)
    return y.reshape(B, S, H * D) * g.astype(jnp.float32)


def causal_dwconv(u, w, b):
    C = u.shape[-1]
    out = lax.conv_general_dilated(u, w.astype(u.dtype)[:, None, :], window_strides=(1,),
                                   padding=[(CONV_WIDTH - 1, 0)],
                                   dimension_numbers=("NWC", "WIO", "NWC"),
                                   feature_group_count=C)
    return out + b.astype(u.dtype)


def split_heads(t, H):
    B, S, _ = t.shape
    return t.reshape(B, S, H, HEAD_DIM)


def rotary(t, positions):
    D = t.shape[-1]
    inv_freq = ROPE_BASE ** (-jnp.arange(0, D, 2, dtype=jnp.float32) / D)
    ang = positions.astype(jnp.float32)[..., None] * inv_freq
    cos, sin = jnp.cos(ang)[:, :, None, :], jnp.sin(ang)[:, :, None, :]
    t1, t2 = t[..., : D // 2], t[..., D // 2:]
    return jnp.concatenate([t1 * cos - t2 * sin, t2 * cos + t1 * sin], axis=-1)


def to_chunks(t):
    B, S, H, D = t.shape
    return t.reshape(B, S // CHUNK, CHUNK, H, D).transpose(0, 3, 1, 2, 4)


def gate_chunks(t):
    B, S, H = t.shape
    return t.reshape(B, S // CHUNK, CHUNK, H).transpose(0, 3, 1, 2)


def from_chunks(t):
    B, H, NC, L, D = t.shape
    return t.transpose(0, 2, 3, 1, 4).reshape(B, NC * L, H, D)


def mlstm_chunkwise(q, k, v, i_pre, f_pre):
    B, S, H, D = q.shape
    q, k, v = to_chunks(q), to_chunks(k) * (D ** -0.5), to_chunks(v)
    ig = gate_chunks(i_pre)
    b = jnp.cumsum(gate_chunks(jax.nn.log_sigmoid(f_pre)), axis=-1)
    b_last = b[..., -1]
    w_end = b_last[..., None] - b + ig

    def step(carry, inp):
        C, n, m = carry
        kc, vc, wc, bl = inp
        m_new = jnp.maximum(bl + m, wc.max(-1))
        decay = jnp.exp(bl + m - m_new)
        a = jnp.exp(wc - m_new[..., None])[..., None] * kc
        C_new = decay[..., None, None] * C + jnp.einsum("bhld,bhle->bhde", a, vc)
        n_new = decay[..., None] * n + a.sum(axis=2)
        return (C_new, n_new, m_new), (C, n, m)

    init = (jnp.zeros((B, H, D, D), jnp.float32), jnp.zeros((B, H, D), jnp.float32),
            jnp.zeros((B, H), jnp.float32))
    xs = (jnp.moveaxis(k, 2, 0), jnp.moveaxis(v, 2, 0), jnp.moveaxis(w_end, 2, 0), jnp.moveaxis(b_last, 2, 0))
    _, (C0, n0, m0) = lax.scan(step, init, xs)
    C0 = jnp.moveaxis(C0, 0, 2)
    n0 = jnp.moveaxis(n0, 0, 2)
    m0 = jnp.moveaxis(m0, 0, 2)

    causal = jnp.tril(jnp.ones((CHUNK, CHUNK), dtype=bool))
    log_d = jnp.where(causal, b[..., :, None] - b[..., None, :] + ig[..., None, :], -jnp.inf)
    log_inter = b + m0[..., None]
    m_row = jnp.maximum(log_inter, log_d.max(-1))
    dmat = jnp.exp(log_d - m_row[..., None])
    inter_scale = jnp.exp(log_inter - m_row)
    s = jnp.einsum("bhcld,bhcsd->bhcls", q, k) * dmat
    num = inter_scale[..., None] * jnp.einsum("bhcld,bhcde->bhcle", q, C0) + jnp.einsum("bhcls,bhcse->bhcle", s, v)
    den = inter_scale * jnp.einsum("bhcld,bhcd->bhcl", q, n0) + s.sum(-1)
    h = num / jnp.maximum(jnp.abs(den), jnp.exp(-m_row))[..., None]
    return from_chunks(h)


def retention_chunkwise(q, k, v):
    B, S, H, D = q.shape
    log_gamma = jnp.log(1.0 - jnp.exp2(-5.0 - jnp.arange(H, dtype=jnp.float32)))
    q, k, v = to_chunks(q), to_chunks(k) * (D ** -0.5), to_chunks(v)
    idx = jnp.arange(CHUNK, dtype=jnp.float32)
    rel = idx[:, None] - idx[None, :]
    decay_mask = jnp.where(rel >= 0, jnp.exp(log_gamma[:, None, None] * jnp.maximum(rel, 0.0)), 0.0)
    scores = jnp.einsum("bhcld,bhcsd->bhcls", q, k) * decay_mask[None, :, None]
    o_intra = jnp.einsum("bhcls,bhcse->bhcle", scores, v)

    key_decay = jnp.exp(log_gamma[:, None] * (CHUNK - 1 - idx))
    U = jnp.einsum("bhcld,bhcle->bhcde", k * key_decay[None, :, None, :, None], v)
    chunk_decay = jnp.exp(log_gamma * CHUNK)[None, :, None, None]

    def step(R, u):
        return chunk_decay * R + u, R

    _, R0 = lax.scan(step, jnp.zeros((B, H, D, D), jnp.float32), jnp.moveaxis(U, 2, 0))
    R0 = jnp.moveaxis(R0, 0, 2)
    query_decay = jnp.exp(log_gamma[:, None] * (idx + 1.0))
    o_inter = jnp.einsum("bhcld,bhcde->bhcle", q * query_decay[None, :, None, :, None], R0)
    return from_chunks(o_intra + o_inter)


def setup_inputs(seed: int = 0) -> dict:
    key = jax.random.key(seed)
    ks = jax.random.split(key, 20)
    f32 = jnp.float32

    def nrm(k, shape, scale):
        return jax.random.normal(k, shape, f32) * scale

    def gain(k, shape):
        return 1.0 + 0.02 * jax.random.normal(k, shape, f32)

    x = jax.random.normal(ks[0], (BATCH, SEQ, D_MODEL), f32)
    start = jax.random.randint(ks[1], (BATCH, 1), 0, 1024, dtype=jnp.int32)
    positions = (start + jnp.arange(SEQ, dtype=jnp.int32)[None, :]).astype(jnp.int32)
    b_fgate = jnp.linspace(3.0, 6.0, N_MLSTM_HEADS, dtype=f32)[None, :] + nrm(ks[7], (DEPTH, N_MLSTM_HEADS), 0.01)
    return {
        "x": x,
        "positions": positions,
        "g_mix_pre": gain(ks[2], (DEPTH, D_MODEL)),
        "w_in": nrm(ks[3], (DEPTH, D_MODEL, D_PROJ), D_MODEL ** -0.5),
        "conv_w": nrm(ks[4], (DEPTH, CONV_WIDTH, 2 * D_MLSTM), CONV_WIDTH ** -0.5),
        "conv_b": nrm(ks[5], (DEPTH, 2 * D_MLSTM), 0.02),
        "b_igate": nrm(ks[6], (DEPTH, N_MLSTM_HEADS), 0.1),
        "b_fgate": b_fgate,
        "g_mlstm_head": gain(ks[8], (DEPTH, D_MLSTM)),
        "g_ret_head": gain(ks[9], (DEPTH, D_RET)),
        "w_out": nrm(ks[10], (DEPTH, D_MIX, D_MODEL), D_MIX ** -0.5),
        "g_mix_post": gain(ks[11], (DEPTH, D_MODEL)),
        "g_ffn_pre": gain(ks[12], (DEPTH, D_MODEL)),
        "w_gate": nrm(ks[13], (DEPTH, D_MODEL, D_FF), D_MODEL ** -0.5),
        "w_up": nrm(ks[14], (DEPTH, D_MODEL, D_FF), D_MODEL ** -0.5),
        "w_down": nrm(ks[15], (DEPTH, D_FF, D_MODEL), D_FF ** -0.5),
        "g_ffn_post": gain(ks[16], (DEPTH, D_MODEL)),
    }


def reference(x, positions, g_mix_pre, w_in, conv_w, conv_b, b_igate, b_fgate, g_mlstm_head, g_ret_head,
              w_out, g_mix_post, g_ffn_pre, w_gate, w_up, w_down, g_ffn_post):
    f32 = jnp.float32
    for l in range(DEPTH):
        h = rmsnorm(x, g_mix_pre[l])
        proj = h @ w_in[l]
        mq, mk, mv, mo, mi, mf, rq, rk, rv, rg = jnp.split(proj, SPLITS, axis=-1)

        qk = jax.nn.silu(causal_dwconv(jnp.concatenate([mq, mk], axis=-1), conv_w[l], conv_b[l]))
        mq, mk = qk[..., :D_MLSTM], qk[..., D_MLSTM:]
        hm = mlstm_chunkwise(split_heads(mq, N_MLSTM_HEADS).astype(f32),
                             split_heads(mk, N_MLSTM_HEADS).astype(f32),
                             split_heads(mv, N_MLSTM_HEADS).astype(f32),
                             (mi + b_igate[l]).astype(f32), (mf + b_fgate[l]).astype(f32))
        hm = head_rmsnorm(hm, g_mlstm_head[l]) * jax.nn.sigmoid(mo.astype(f32))

        rq_h = rotary(split_heads(rq, N_RET_HEADS).astype(f32), positions)
        rk_h = rotary(split_heads(rk, N_RET_HEADS).astype(f32), positions)
        hr = retention_chunkwise(rq_h, rk_h, split_heads(rv, N_RET_HEADS).astype(f32))
        hr = head_rmsnorm(hr, g_ret_head[l]) * jax.nn.silu(rg.astype(f32))

        mix = jnp.concatenate([hm, hr], axis=-1).astype(x.dtype) @ w_out[l]
        x = x + rmsnorm(mix, g_mix_post[l])

        h = rmsnorm(x, g_ffn_pre[l])
        ff = (jax.nn.silu(h @ w_gate[l]) * (h @ w_up[l])) @ w_down[l]
        x = x + rmsnorm(ff, g_ffn_post[l])
    return x
```

```python
import functools
import math

import jax
import jax.numpy as jnp
from jax import lax
from jax.experimental import pallas as pl
from jax.experimental.pallas import tpu as pltpu

F32 = jnp.float32
BF16 = jnp.bfloat16

D_MODEL = 2048
HEAD_DIM = 128
N_HEADS = 8
D_GROUP = N_HEADS * HEAD_DIM
D_MAIN = 8 * D_GROUP
N_GATES = 2 * N_HEADS
CHUNK = 128
CONV_WIDTH = 4
ROPE_BASE = 10000.0
NORM_EPS = 1e-6
K_SCALE = HEAD_DIM ** -0.5
LOG_GAMMA = tuple(math.log(1.0 - 2.0 ** (-5.0 - h)) for h in range(N_HEADS))

LANES = 128
VMEM_LIMIT = 56 * 1024 * 1024


def _rms(v):
    return v * lax.rsqrt(jnp.mean(v * v, axis=-1, keepdims=True) + NORM_EPS)


def _inproj_kernel(x_ref, g_ref, w_ref, wg_ref, proj_ref, gates_ref, h_ref):
    @pl.when(pl.program_id(1) == 0)
    def _():
        h = (_rms(x_ref[...]) * g_ref[...]).astype(BF16)
        h_ref[...] = h
        gt = lax.dot_general(wg_ref[...], h, (((1,), (1,)), ((), ())),
                             preferred_element_type=F32)
        for c in range(gates_ref.shape[0]):
            gates_ref[c] = gt[:, c * CHUNK:(c + 1) * CHUNK]

    proj_ref[...] = jnp.dot(h_ref[...], w_ref[...],
                            preferred_element_type=F32).astype(BF16)


def _inproj(x, g, w_main, w_gt, *, tm, tn):
    s = x.shape[0]
    nc = s // CHUNK
    return pl.pallas_call(
        _inproj_kernel,
        grid=(s // tm, D_MAIN // tn),
        in_specs=[
            pl.BlockSpec((tm, D_MODEL), lambda i, j: (i, 0)),
            pl.BlockSpec((1, D_MODEL), lambda i, j: (0, 0)),
            pl.BlockSpec((D_MODEL, tn), lambda i, j: (0, j)),
            pl.BlockSpec((N_GATES, D_MODEL), lambda i, j: (0, 0)),
        ],
        out_specs=[
            pl.BlockSpec((tm, tn), lambda i, j: (i, j)),
            pl.BlockSpec((tm // CHUNK, N_GATES, CHUNK), lambda i, j: (i, 0, 0)),
        ],
        out_shape=[
            jax.ShapeDtypeStruct((s, D_MAIN), BF16),
            jax.ShapeDtypeStruct((nc, N_GATES, CHUNK), F32),
        ],
        scratch_shapes=[pltpu.VMEM((tm, D_MODEL), BF16)],
        compiler_params=pltpu.CompilerParams(
            dimension_semantics=("arbitrary", "arbitrary"),
            vmem_limit_bytes=VMEM_LIMIT),
        name="inproj",
    )(x, g, w_main, w_gt)


def _lane_scan(v, lane, op, identity):
    for d in (1, 2, 4, 8, 16, 32, 64):
        v = op(v, jnp.where(lane >= d, pltpu.roll(v, d, 1), identity))
    return v


def _gate_kernel(gates_ref, bias_ref, grow_ref, arow_ref, drow_ref, colp_ref,
                 bl_s, gmax_s, m0_s, mnew_s, m_s, inter_s, floor_s, tile_s):
    nc = gates_ref.shape[0]
    rows = nc * N_HEADS
    gts = gates_ref[...]
    bias = bias_ref[...]
    ig = (gts[:, 0:N_HEADS, :] + bias[None, 0:N_HEADS, :]).reshape(rows, CHUNK)
    fpre = (gts[:, N_HEADS:, :] + bias[None, N_HEADS:, :]).reshape(rows, CHUNK)
    fg = jnp.minimum(fpre, 0.0) - jnp.log1p(jnp.exp(-jnp.abs(fpre)))
    lane = lax.broadcasted_iota(jnp.int32, (rows, CHUNK), 1)
    b = _lane_scan(fg, lane, jnp.add, 0.0)
    g = ig - b
    cm = _lane_scan(g, lane, jnp.maximum, -jnp.inf)
    bl = jnp.broadcast_to(b[:, CHUNK - 1:CHUNK], (rows, CHUNK))
    bl_s[...] = bl
    gmax_s[...] = jnp.broadcast_to(cm[:, CHUNK - 1:CHUNK], (rows, CHUNK))

    def scan_body(c, m):
        r = pl.multiple_of(c * N_HEADS, N_HEADS)
        m0_s[pl.ds(r, N_HEADS), :] = m
        m_new = bl_s[pl.ds(r, N_HEADS), :] + jnp.maximum(m, gmax_s[pl.ds(r, N_HEADS), :])
        mnew_s[pl.ds(r, N_HEADS), :] = m_new
        return m_new

    lax.fori_loop(0, nc, scan_body, jnp.zeros((N_HEADS, CHUNK), F32))

    m0 = m0_s[...]
    mnew = mnew_s[...]
    mm = jnp.maximum(m0, cm)
    grow_ref[...] = g
    arow_ref[...] = jnp.exp(g + bl - mnew)
    drow_ref[...] = jnp.exp(bl + m0 - mnew)
    m_s[...] = mm
    inter_s[...] = jnp.exp(m0 - mm)
    floor_s[...] = jnp.exp(-b - mm)

    tile_s[...] = jnp.zeros_like(tile_s)

    def tr_body(c, carry):
        r = pl.multiple_of(c * N_HEADS, N_HEADS)
        tile_s[0:N_HEADS, :] = m_s[pl.ds(r, N_HEADS), :]
        tile_s[N_HEADS:2 * N_HEADS, :] = inter_s[pl.ds(r, N_HEADS), :]
        tile_s[2 * N_HEADS:3 * N_HEADS, :] = floor_s[pl.ds(r, N_HEADS), :]
        colp_ref[c] = tile_s[...].T
        return carry

    lax.fori_loop(0, nc, tr_body, 0)


def _gates(gates, bias):
    nc = gates.shape[0]
    rows = nc * N_HEADS
    row_arr = jax.ShapeDtypeStruct((rows, CHUNK), F32)
    return pl.pallas_call(
        _gate_kernel,
        out_shape=[row_arr, row_arr, row_arr,
                   jax.ShapeDtypeStruct((nc, CHUNK, LANES), F32)],
        scratch_shapes=[pltpu.VMEM((rows, CHUNK), F32)] * 7
        + [pltpu.VMEM((CHUNK, LANES), F32)],
        compiler_params=pltpu.CompilerParams(vmem_limit_bytes=VMEM_LIMIT),
        name="gates",
    )(gates, bias)


def _mixer_kernel(mq_ref, mk_ref, mv_ref, mo_ref, rq_ref, rk_ref, rv_ref, rg_ref,
                  pos_ref, invf_ref, cw_ref, cb_ref, gm_ref, gr_ref,
                  grow_ref, arow_ref, drow_ref, colp_ref,
                  mix_ref,
                  qk_s, cos_s, sin_s, c_s, r_s, dmask_s, qd_s, kd_s, carry_s):
    tb = mq_ref.shape[0]
    ncb = tb // CHUNK
    row = lax.broadcasted_iota(jnp.int32, (CHUNK, CHUNK), 0)
    col = lax.broadcasted_iota(jnp.int32, (CHUNK, CHUNK), 1)

    @pl.when(pl.program_id(0) == 0)
    def _init():
        c_s[...] = jnp.zeros_like(c_s)
        r_s[...] = jnp.zeros_like(r_s)
        carry_s[...] = jnp.zeros_like(carry_s)
        rel = (row - col).astype(F32)
        for h in range(N_HEADS):
            lg = LOG_GAMMA[h]
            dmask_s[h] = jnp.where(rel >= 0.0, jnp.exp(lg * jnp.maximum(rel, 0.0)), 0.0)
            qd_s[h] = jnp.exp(lg * (row.astype(F32) + 1.0))
            kd_s[h] = jnp.exp(lg * (CHUNK - 1.0 - col.astype(F32)))

    ang = pos_ref[...].astype(F32) * invf_ref[...]
    lane = lax.broadcasted_iota(jnp.int32, ang.shape, 1)
    cos_s[...] = jnp.cos(ang)
    sn = jnp.sin(ang)
    sin_s[...] = jnp.where(lane < HEAD_DIM // 2, -sn, sn)

    for sl in range(2 * N_HEADS):
        src = mq_ref if sl < N_HEADS else mk_ref
        cs = slice((sl % N_HEADS) * LANES, (sl % N_HEADS + 1) * LANES)
        gs = slice(sl * LANES, (sl + 1) * LANES)
        u = src[:, cs].astype(F32)
        ext = jnp.concatenate([carry_s[:, gs], u], axis=0)
        y = cb_ref[:, gs] + cw_ref[CONV_WIDTH - 1:CONV_WIDTH, gs] * u
        for j in range(1, CONV_WIDTH):
            wj = cw_ref[CONV_WIDTH - 1 - j:CONV_WIDTH - j, gs]
            y = y + wj * pltpu.roll(ext, j, 0)[8:]
        y = y * jax.nn.sigmoid(y)
        if sl >= N_HEADS:
            y = y * K_SCALE
        qk_s[:, gs] = y.astype(BF16)
        carry_s[:, gs] = u[tb - 8:]

    causal = row >= col
    onecol = jnp.where(col == 0, 1.0, 0.0).astype(BF16)

    def chunk_body(c, carry):
        r0 = pl.multiple_of(c * CHUNK, CHUNK)
        rows = pl.ds(r0, CHUNK)
        tile = colp_ref[c]
        cosb = cos_s[rows, :]
        sinb = sin_s[rows, :]
        for h in range(N_HEADS):
            hs = slice(h * HEAD_DIM, (h + 1) * HEAD_DIM)
            ks = slice(D_GROUP + h * HEAD_DIM, D_GROUP + (h + 1) * HEAD_DIM)
            q = qk_s[rows, hs]
            kt = qk_s[rows, ks].astype(F32).T
            vaug = jnp.concatenate([mv_ref[rows, hs], onecol], axis=1)
            gr = grow_ref[pl.ds(c * N_HEADS + h, 1), :]
            ar = arow_ref[pl.ds(c * N_HEADS + h, 1), :]
            dr = drow_ref[pl.ds(c * N_HEADS + h, 1), :]
            m_col = tile[:, h:h + 1]
            inter = tile[:, N_HEADS + h:N_HEADS + h + 1]
            floor = tile[:, 2 * N_HEADS + h:2 * N_HEADS + h + 1]
            dmat = jnp.where(causal, jnp.exp(gr - m_col), 0.0)
            s = jnp.dot(q, kt.astype(BF16), preferred_element_type=F32) * dmat
            c_old = c_s[h]
            y = inter * jnp.dot(q, c_old.astype(BF16), preferred_element_type=F32) \
                + jnp.dot(s.astype(BF16), vaug, preferred_element_type=F32)
            den = y[:, HEAD_DIM:HEAD_DIM + 1]
            hm = y[:, :HEAD_DIM] / jnp.maximum(jnp.abs(den), floor)
            at = (kt * ar).astype(BF16)
            dr2 = jnp.concatenate([dr, dr], axis=1)
            c_s[h] = dr2 * c_old + jnp.dot(at, vaug, preferred_element_type=F32)
            hm = _rms(hm) * gm_ref[:, hs] * jax.nn.sigmoid(mo_ref[rows, hs].astype(F32))
            mix_ref[rows, hs] = hm.astype(BF16)
            rq = rq_ref[rows, hs].astype(F32)
            rk = rk_ref[rows, hs].astype(F32)
            rv = rv_ref[rows, hs]
            qr = rq * cosb + pltpu.roll(rq, HEAD_DIM // 2, 1) * sinb
            kr = (rk * cosb + pltpu.roll(rk, HEAD_DIM // 2, 1) * sinb) * K_SCALE
            krt = kr.T
            qrb = qr.astype(BF16)
            sc = jnp.dot(qrb, krt.astype(BF16), preferred_element_type=F32) * dmask_s[h]
            r_old = r_s[h]
            hr = jnp.dot(sc.astype(BF16), rv, preferred_element_type=F32) \
                + jnp.dot((qr * qd_s[h]).astype(BF16), r_old.astype(BF16),
                          preferred_element_type=F32)
            r_s[h] = math.exp(LOG_GAMMA[h] * CHUNK) * r_old \
                + jnp.dot((krt * kd_s[h]).astype(BF16), rv, preferred_element_type=F32)
            g = rg_ref[rows, hs].astype(F32)
            hr = _rms(hr) * gr_ref[:, hs] * (g * jax.nn.sigmoid(g))
            mix_ref[rows, ks] = hr.astype(BF16)
        return carry

    lax.fori_loop(0, ncb, chunk_body, 0)


def _mixer(proj, pos, invf, conv_w, conv_b, g_m, g_r, grow, arow, drow, colp, *, tb):
    s = proj.shape[0]
    ncb = tb // CHUNK

    def sec(k):
        return pl.BlockSpec((tb, D_GROUP), lambda t, k=k: (t, k))

    def whole(shape):
        return pl.BlockSpec(shape, lambda t: (0,) * len(shape))

    row_spec = pl.BlockSpec((ncb * N_HEADS, CHUNK), lambda t: (t, 0))
    return pl.pallas_call(
        _mixer_kernel,
        grid=(s // tb,),
        in_specs=[sec(k) for k in range(8)] + [
            pl.BlockSpec((tb, 1), lambda t: (t, 0)),
            whole((1, LANES)),
            whole((CONV_WIDTH, 2 * D_GROUP)),
            whole((1, 2 * D_GROUP)),
            whole((1, D_GROUP)),
            whole((1, D_GROUP)),
            row_spec, row_spec, row_spec,
            pl.BlockSpec((ncb, CHUNK, LANES), lambda t: (t, 0, 0)),
        ],
        out_specs=pl.BlockSpec((tb, 2 * D_GROUP), lambda t: (t, 0)),
        out_shape=jax.ShapeDtypeStruct((s, 2 * D_GROUP), BF16),
        scratch_shapes=[
            pltpu.VMEM((tb, 2 * D_GROUP), BF16),
            pltpu.VMEM((tb, LANES), F32),
            pltpu.VMEM((tb, LANES), F32),
            pltpu.VMEM((N_HEADS, HEAD_DIM, 2 * HEAD_DIM), F32),
            pltpu.VMEM((N_HEADS, HEAD_DIM, HEAD_DIM), F32),
            pltpu.VMEM((N_HEADS, CHUNK, CHUNK), F32),
            pltpu.VMEM((N_HEADS, CHUNK, HEAD_DIM), F32),
            pltpu.VMEM((N_HEADS, HEAD_DIM, CHUNK), F32),
            pltpu.VMEM((8, 2 * D_GROUP), F32),
        ],
        compiler_params=pltpu.CompilerParams(
            dimension_semantics=("arbitrary",),
            vmem_limit_bytes=VMEM_LIMIT),
        name="mixer",
    )(*([proj] * 8), pos, invf, conv_w, conv_b, g_m, g_r, grow, arow, drow, colp)


def _outproj_kernel(mix_ref, w_ref, x_ref, gpost_ref, gpre_ref, x1_ref, h2_ref):
    y = jnp.dot(mix_ref[...], w_ref[...], preferred_element_type=F32)
    x1 = x_ref[...] + _rms(y) * gpost_ref[...]
    x1_ref[...] = x1
    h2_ref[...] = (_rms(x1) * gpre_ref[...]).astype(BF16)


def _outproj(mix, w_out, x, g_post, g_pre, *, tm):
    s = x.shape[0]
    return pl.pallas_call(
        _outproj_kernel,
        grid=(s // tm,),
        in_specs=[
            pl.BlockSpec((tm, D_MODEL), lambda i: (i, 0)),
            pl.BlockSpec((D_MODEL, D_MODEL), lambda i: (0, 0)),
            pl.BlockSpec((tm, D_MODEL), lambda i: (i, 0)),
            pl.BlockSpec((1, D_MODEL), lambda i: (0, 0)),
            pl.BlockSpec((1, D_MODEL), lambda i: (0, 0)),
        ],
        out_specs=[
            pl.BlockSpec((tm, D_MODEL), lambda i: (i, 0)),
            pl.BlockSpec((tm, D_MODEL), lambda i: (i, 0)),
        ],
        out_shape=[
            jax.ShapeDtypeStruct((s, D_MODEL), F32),
            jax.ShapeDtypeStruct((s, D_MODEL), BF16),
        ],
        compiler_params=pltpu.CompilerParams(
            dimension_semantics=("arbitrary",),
            vmem_limit_bytes=VMEM_LIMIT),
        name="outproj",
    )(mix, w_out, x, g_post, g_pre)


def _ffn_kernel(h_ref, wg_ref, wu_ref, wd_ref, x1_ref, gpost_ref, o_ref, acc_ref):
    j = pl.program_id(1)
    h = h_ref[...]
    g = jnp.dot(h, wg_ref[...], preferred_element_type=F32)
    u = jnp.dot(h, wu_ref[...], preferred_element_type=F32)
    a = (g * jax.nn.sigmoid(g) * u).astype(BF16)
    p = jnp.dot(a, wd_ref[...], preferred_element_type=F32)

    @pl.when(j == 0)
    def _():
        acc_ref[...] = p

    @pl.when(j > 0)
    def _():
        acc_ref[...] += p

    @pl.when(j == pl.num_programs(1) - 1)
    def _():
        o_ref[...] = x1_ref[...] + _rms(acc_ref[...]) * gpost_ref[...]


def _ffn(h2, w_gate, w_up, w_down, x1, g_post, *, tm, tf):
    s = h2.shape[0]
    d_ff = w_gate.shape[1]
    return pl.pallas_call(
        _ffn_kernel,
        grid=(s // tm, d_ff // tf),
        in_specs=[
            pl.BlockSpec((tm, D_MODEL), lambda i, j: (i, 0)),
            pl.BlockSpec((D_MODEL, tf), lambda i, j: (0, j)),
            pl.BlockSpec((D_MODEL, tf), lambda i, j: (0, j)),
            pl.BlockSpec((tf, D_MODEL), lambda i, j: (j, 0)),
            pl.BlockSpec((tm, D_MODEL), lambda i, j: (i, 0)),
            pl.BlockSpec((1, D_MODEL), lambda i, j: (0, 0)),
        ],
        out_specs=pl.BlockSpec((tm, D_MODEL), lambda i, j: (i, 0)),
        out_shape=jax.ShapeDtypeStruct((s, D_MODEL), F32),
        scratch_shapes=[pltpu.VMEM((tm, D_MODEL), F32)],
        compiler_params=pltpu.CompilerParams(
            dimension_semantics=("arbitrary", "arbitrary"),
            vmem_limit_bytes=VMEM_LIMIT),
        name="ffn",
    )(h2, w_gate, w_up, w_down, x1, g_post)


def _layer(x, pos, g_mix_pre, w_in, conv_w, conv_b, b_igate, b_fgate, g_mlstm_head,
           g_ret_head, w_out, g_mix_post, g_ffn_pre, w_gate, w_up, w_down, g_ffn_post):
    n_m = 4 * D_GROUP
    w_main = jnp.concatenate([w_in[:, :n_m], w_in[:, n_m + N_GATES:]], axis=1).astype(BF16)
    w_gt = w_in[:, n_m:n_m + N_GATES].T.astype(BF16)
    bias = jnp.broadcast_to(jnp.concatenate([b_igate, b_fgate])[:, None], (N_GATES, CHUNK))
    half = ROPE_BASE ** (-jnp.arange(0, HEAD_DIM, 2, dtype=F32) / HEAD_DIM)
    invf = jnp.concatenate([half, half])[None, :]

    def r2(v):
        return v.reshape(1, -1)

    proj, gates = _inproj(x, r2(g_mix_pre), w_main, w_gt, tm=1024, tn=512)
    grow, arow, drow, colp = _gates(gates, bias)
    mix = _mixer(proj, pos, invf, conv_w, r2(conv_b), r2(g_mlstm_head), r2(g_ret_head),
                 grow, arow, drow, colp, tb=512)
    x1, h2 = _outproj(mix, w_out.astype(BF16), x, r2(g_mix_post), r2(g_ffn_pre), tm=256)
    return _ffn(h2, w_gate.astype(BF16), w_up.astype(BF16), w_down.astype(BF16), x1,
                r2(g_ffn_post), tm=512, tf=512)


def kernel(x, positions, g_mix_pre, w_in, conv_w, conv_b, b_igate, b_fgate, g_mlstm_head,
           g_ret_head, w_out, g_mix_post, g_ffn_pre, w_gate, w_up, w_down, g_ffn_post):
    batch, seq, _ = x.shape
    outs = []
    for b in range(batch):
        xb = x[b]
        pos = positions[b].reshape(seq, 1)
        for l in range(w_in.shape[0]):
            xb = _layer(xb, pos, g_mix_pre[l], w_in[l], conv_w[l], conv_b[l], b_igate[l],
                        b_fgate[l], g_mlstm_head[l], g_ret_head[l], w_out[l], g_mix_post[l],
                        g_ffn_pre[l], w_gate[l], w_up[l], w_down[l], g_ffn_post[l])
        outs.append(xb)
    return jnp.stack(outs)
```
